```python
import jax, jax.numpy as jnp
from jax import lax
import numpy as np

D_MODEL = 1024
BATCH = 4
SEQ = 4096
DEPTH = 2

D_MIX = D_MODEL
HEAD_DIM = 64
SGU_DIM = 3 * D_MIX // 8
SGU_HEADS = SGU_DIM // HEAD_DIM
CONV_DIM = 3 * D_MIX // 8
POOL_DIM = D_MIX - SGU_DIM - CONV_DIM
POOL_WINDOWS = (2, 4, 8, 16)
POOL_GROUPS = len(POOL_WINDOWS)
POOL_GROUP_DIM = POOL_DIM // POOL_GROUPS
CHUNK = 128
CONV_WIDTH = 31
D_IN = 2 * SGU_DIM + 2 * CONV_DIM + POOL_DIM
D_FF = 2816
FFN_RESIDUAL_WEIGHT = 0.5
EPS = 1e-6

kernel_name = "hybrid_macaron_sgu_conv_pool"


def rms_norm(x, g):
    xf = x.astype(jnp.float32)
    y = xf * lax.rsqrt(jnp.mean(xf * xf, axis=-1, keepdims=True) + EPS)
    return (y * g.astype(jnp.float32)).astype(x.dtype)


def layer_norm(x, g, b):
    xf = x.astype(jnp.float32)
    mu = jnp.mean(xf, axis=-1, keepdims=True)
    xc = xf - mu
    var = jnp.mean(xc * xc, axis=-1, keepdims=True)
    y = xc * lax.rsqrt(var + EPS) * g.astype(jnp.float32) + b.astype(jnp.float32)
    return y.astype(x.dtype)


def swiglu_ffn(h, w_gate, w_up, w_down):
    return (jax.nn.silu(h @ w_gate) * (h @ w_up)) @ w_down


def spatial_gating(u, v, ln_g, ln_b, w_s, b_s):
    bsz, t_len, _ = v.shape
    v = layer_norm(v, ln_g, ln_b)
    causal = jnp.tril(jnp.ones((CHUNK, CHUNK), dtype=bool))
    w = jnp.where(causal[None], w_s, jnp.zeros_like(w_s))
    vc = v.reshape(bsz, t_len // CHUNK, CHUNK, SGU_HEADS, HEAD_DIM)
    mixed = jnp.einsum('hpq,bnqhc->bnphc', w, vc) + b_s.T[None, None, :, :, None]
    return u * mixed.reshape(bsz, t_len, SGU_DIM)


def conformer_conv(val, gate, conv_w, conv_b, ln_g, ln_b):
    h = val * jax.nn.sigmoid(gate)
    h = lax.conv_general_dilated(
        h, conv_w[:, None, :].astype(h.dtype), window_strides=(1,),
        padding=((CONV_WIDTH - 1, 0),),
        dimension_numbers=('NWC', 'WIO', 'NWC'),
        feature_group_count=CONV_DIM) + conv_b
    h = layer_norm(h, ln_g, ln_b)
    return jax.nn.silu(h)


def multiscale_pool(p, pool_w, pool_scale):
    bsz, t_len, _ = p.shape
    pg = p.reshape(bsz, t_len, POOL_GROUPS, POOL_GROUP_DIM)
    csum = jnp.cumsum(pg.astype(jnp.float32), axis=1)
    pos = jnp.arange(1, t_len + 1, dtype=jnp.float32)
    means = []
    for g, w in enumerate(POOL_WINDOWS):
        cg = csum[:, :, g]
        lagged = jnp.pad(cg, ((0, 0), (w, 0), (0, 0)))[:, :t_len]
        count = jnp.minimum(pos, jnp.float32(w))[None, :, None]
        means.append((cg - lagged) / count)
    pooled = jnp.stack(means, axis=2).astype(p.dtype) - pg
    mixed = jnp.einsum('btgc,gcd->btgd', pooled, pool_w).reshape(bsz, t_len, POOL_DIM)
    return mixed * pool_scale


def hybrid_mixer(h, w_in, sgu_ln_g, sgu_ln_b, w_spatial, b_spatial,
                 conv_w, conv_b, conv_ln_g, conv_ln_b, pool_w, pool_scale, w_out):
    z = h @ w_in
    s1 = SGU_DIM
    s2 = 2 * SGU_DIM
    s3 = s2 + CONV_DIM
    s4 = s3 + CONV_DIM
    a_u, a_v, b_val, b_gate, c_in = z[..., :s1], z[..., s1:s2], z[..., s2:s3], z[..., s3:s4], z[..., s4:]
    a = spatial_gating(jax.nn.gelu(a_u, approximate=False), jax.nn.gelu(a_v, approximate=False),
                       sgu_ln_g, sgu_ln_b, w_spatial, b_spatial)
    b = conformer_conv(b_val, b_gate, conv_w, conv_b, conv_ln_g, conv_ln_b)
    c = multiscale_pool(c_in, pool_w, pool_scale)
    return jnp.concatenate([a, b, c], axis=-1) @ w_out


def setup_inputs(seed: int = 0) -> dict:
    key = jax.random.key(seed)
    ks = jax.random.split(key, 24)
    f32 = jnp.float32

    def nrm(k, shape, scale):
        return jax.random.normal(k, shape, f32) * scale

    L, D, F = DEPTH, D_MODEL, D_FF
    return {
        "x": jax.random.normal(ks[0], (BATCH, SEQ, D), f32),
        "ffn1_norm": 1.0 + nrm(ks[1], (L, D), 0.02),
        "ffn1_w_gate": nrm(ks[2], (L, D, F), D ** -0.5),
        "ffn1_w_up": nrm(ks[3], (L, D, F), D ** -0.5),
        "ffn1_w_down": nrm(ks[4], (L, F, D), F ** -0.5),
        "mix_norm": 1.0 + nrm(ks[5], (L, D), 0.02),
        "w_in": nrm(ks[6], (L, D, D_IN), D ** -0.5),
        "sgu_ln_g": 1.0 + nrm(ks[7], (L, SGU_DIM), 0.02),
        "sgu_ln_b": nrm(ks[8], (L, SGU_DIM), 0.02),
        "w_spatial": nrm(ks[9], (L, SGU_HEADS, CHUNK, CHUNK), CHUNK ** -0.5),
        "b_spatial": 1.0 + nrm(ks[10], (L, SGU_HEADS, CHUNK), 0.02),
        "conv_w": nrm(ks[11], (L, CONV_WIDTH, CONV_DIM), CONV_WIDTH ** -0.5),
        "conv_b": nrm(ks[12], (L, CONV_DIM), 0.02),
        "conv_ln_g": 1.0 + nrm(ks[13], (L, CONV_DIM), 0.02),
        "conv_ln_b": nrm(ks[14], (L, CONV_DIM), 0.02),
        "pool_w": nrm(ks[15], (L, POOL_GROUPS, POOL_GROUP_DIM, POOL_GROUP_DIM), POOL_GROUP_DIM ** -0.5),
        "pool_scale": 1.0 + nrm(ks[16], (L, POOL_DIM), 0.1),
        "w_out": nrm(ks[17], (L, D_MIX, D), D_MIX ** -0.5),
        "ffn2_norm": 1.0 + nrm(ks[18], (L, D), 0.02),
        "ffn2_w_gate": nrm(ks[19], (L, D, F), D ** -0.5),
        "ffn2_w_up": nrm(ks[20], (L, D, F), D ** -0.5),
        "ffn2_w_down": nrm(ks[21], (L, F, D), F ** -0.5),
        "final_norm": 1.0 + nrm(ks[22], (D,), 0.02),
    }


def reference(x, ffn1_norm, ffn1_w_gate, ffn1_w_up, ffn1_w_down, mix_norm, w_in,
              sgu_ln_g, sgu_ln_b, w_spatial, b_spatial, conv_w, conv_b, conv_ln_g, conv_ln_b,
              pool_w, pool_scale, w_out, ffn2_norm, ffn2_w_gate, ffn2_w_up, ffn2_w_down,
              final_norm):
    for l in range(DEPTH):
        x = x + FFN_RESIDUAL_WEIGHT * swiglu_ffn(rms_norm(x, ffn1_norm[l]),
                                                 ffn1_w_gate[l], ffn1_w_up[l], ffn1_w_down[l])
        x = x + hybrid_mixer(rms_norm(x, mix_norm[l]), w_in[l],
                             sgu_ln_g[l], sgu_ln_b[l], w_spatial[l], b_spatial[l],
                             conv_w[l], conv_b[l], conv_ln_g[l], conv_ln_b[l],
                             pool_w[l], pool_scale[l], w_out[l])
        x = x + FFN_RESIDUAL_WEIGHT * swiglu_ffn(rms_norm(x, ffn2_norm[l]),
                                                 ffn2_w_gate[l], ffn2_w_up[l], ffn2_w_down[l])
    return rms_norm(x, final_norm)
```

```python
import functools
import math

import jax
import jax.numpy as jnp
from jax import lax
from jax.experimental import pallas as pl
from jax.experimental.pallas import tpu as pltpu

D_MODEL = 1024
D_FF = 2816
HEAD_DIM = 64
SGU_DIM = 384
SGU_HEADS = SGU_DIM // HEAD_DIM
CONV_DIM = 384
POOL_DIM = 256
POOL_WINDOWS = (2, 4, 8, 16)
CHUNK = 128
CONV_WIDTH = 31
D_IN = 2 * SGU_DIM + 2 * CONV_DIM + POOL_DIM
EPS = 1e-6

LANES = 128
SUBLANES = 8
HALO = 32
VMEM_LIMIT_BYTES = 56 * 1024 * 1024

FFN_ROWS = 512
MIX_ROWS = 256

_F32 = jnp.float32
_BF16 = jnp.bfloat16


def _rms_norm(x, g):
    y = x * lax.rsqrt(jnp.mean(x * x, axis=-1, keepdims=True) + EPS)
    return y * g


def _layer_norm(x, g, b):
    mu = jnp.mean(x, axis=-1, keepdims=True)
    xc = x - mu
    var = jnp.mean(xc * xc, axis=-1, keepdims=True)
    return xc * lax.rsqrt(var + EPS) * g + b


def _gelu(x):
    return 0.5 * x * (1.0 + lax.erf(x * math.sqrt(0.5)))


def _silu(x):
    return x * jax.nn.sigmoid(x)


def _dot(a, b):
    return jnp.dot(a, b, preferred_element_type=_F32)


def _ffn_kernel(x_ref, g_ref, wg_ref, wu_ref, wd_ref, *rest, final_norm):
    if final_norm:
        fg_ref, o_ref = rest
    else:
        (o_ref,) = rest
    x = x_ref[...]
    h = _rms_norm(x, g_ref[...]).astype(_BF16)
    gate = _dot(h, wg_ref[...])
    up = _dot(h, wu_ref[...])
    act = (_silu(gate) * up).astype(_BF16)
    y = x + 0.5 * _dot(act, wd_ref[...])
    if final_norm:
        y = _rms_norm(y, fg_ref[...])
    o_ref[...] = y


def _resident(shape):
    return pl.BlockSpec(shape, lambda *_: (0,) * len(shape), pipeline_mode=pl.Buffered(1))


def _ffn(x2d, norm_g, w_gate, w_up, w_down, final_g=None):
    n, d = x2d.shape
    f = w_gate.shape[1]
    assert n % FFN_ROWS == 0
    row_spec = pl.BlockSpec((FFN_ROWS, d), lambda i: (i, 0))
    in_specs = [row_spec, _resident((1, d)), _resident((d, f)), _resident((d, f)), _resident((f, d))]
    args = [x2d, norm_g.reshape(1, d), w_gate, w_up, w_down]
    if final_g is not None:
        in_specs.append(_resident((1, d)))
        args.append(final_g.reshape(1, d))
    return pl.pallas_call(
        functools.partial(_ffn_kernel, final_norm=final_g is not None),
        out_shape=jax.ShapeDtypeStruct((n, d), x2d.dtype),
        grid=(n // FFN_ROWS,),
        in_specs=in_specs,
        out_specs=row_spec,
        compiler_params=pltpu.CompilerParams(
            dimension_semantics=("arbitrary",), vmem_limit_bytes=VMEM_LIMIT_BYTES),
        name="ffn_final" if final_g is not None else "ffn",
    )(*args)


def _mixer_kernel(x_ref, g_ref, win_ref, lng_ref, lnb_ref, ws_ref, bs_ref,
                  cw_ref, cb_ref, clng_ref, clnb_ref, pw_ref, ps_ref, wout_ref,
                  o_ref, conv_ext, pool_ext, s2_ref, s4_ref, s8_ref):
    rows = x_ref.shape[0]
    n_chunks = rows // CHUNK
    step = pl.program_id(1)

    x = x_ref[...]
    hn = _rms_norm(x, g_ref[...]).astype(_BF16)
    z = _dot(hn, win_ref[...])
    s1, s2, s3, s4 = SGU_DIM, 2 * SGU_DIM, 2 * SGU_DIM + CONV_DIM, 2 * SGU_DIM + 2 * CONV_DIM
    a_u, a_v = z[:, :s1], z[:, s1:s2]
    b_val, b_gate, c_in = z[:, s2:s3], z[:, s3:s4], z[:, s4:]

    @pl.when(step == 0)
    def _():
        conv_ext[0:HALO, :] = jnp.zeros((HALO, CONV_DIM), _F32)
        pool_ext[0:HALO, :] = jnp.zeros((HALO, POOL_DIM), _F32)

    u = _gelu(a_u)
    v = _layer_norm(_gelu(a_v), lng_ref[...], lnb_ref[...])
    causal = (lax.broadcasted_iota(jnp.int32, (CHUNK, CHUNK), 0)
              >= lax.broadcasted_iota(jnp.int32, (CHUNK, CHUNK), 1))
    left = lax.broadcasted_iota(jnp.int32, (rows, LANES), 1) < HEAD_DIM
    bias = bs_ref[...]
    mixed_cols = []
    for pair in range(SGU_HEADS // 2):
        vp = v[:, pair * LANES:(pair + 1) * LANES]
        v_l = jnp.where(left, vp, 0.0).astype(_BF16)
        v_r = jnp.where(left, 0.0, vp).astype(_BF16)
        rhs = jnp.concatenate(
            [jnp.concatenate([v_l[n * CHUNK:(n + 1) * CHUNK], v_r[n * CHUNK:(n + 1) * CHUNK]], axis=0)
             for n in range(n_chunks)], axis=1)
        lhs = jnp.concatenate(
            [jnp.where(causal, ws_ref[2 * pair], 0.0), jnp.where(causal, ws_ref[2 * pair + 1], 0.0)],
            axis=1).astype(_BF16)
        out = _dot(lhs, rhs)
        mixed_cols.append(jnp.concatenate(
            [out[:, n * CHUNK:(n + 1) * CHUNK] for n in range(n_chunks)], axis=0))
    mixed = jnp.concatenate(mixed_cols, axis=1) + jnp.concatenate([bias] * n_chunks, axis=0)
    a = u * mixed

    conv_ext[HALO:HALO + rows, :] = b_val * jax.nn.sigmoid(b_gate)
    first = HALO - (CONV_WIDTH - 1)
    conv = None
    for res in range(SUBLANES):
        part = None
        for k in range(CONV_WIDTH):
            if (first + k) % SUBLANES != res:
                continue
            term = conv_ext[first + k:first + k + rows, :] * cw_ref[k:k + 1, :]
            part = term if part is None else part + term
        if part is not None:
            conv = part if conv is None else conv + part
    conv = conv + cb_ref[...]
    b = _silu(_layer_norm(conv, clng_ref[...], clnb_ref[...]))
    conv_ext[0:HALO, :] = conv_ext[rows:rows + HALO, :]

    pool_ext[HALO:HALO + rows, :] = c_in
    n_ext = HALO + rows
    s2_ref[8:n_ext, :] = pool_ext[8:n_ext, :] + pool_ext[7:n_ext - 1, :]
    s4_ref[16:n_ext, :] = s2_ref[16:n_ext, :] + s2_ref[14:n_ext - 2, :]
    s8_ref[24:n_ext, :] = s4_ref[24:n_ext, :] + s4_ref[20:n_ext - 4, :]
    w2 = s2_ref[HALO:n_ext, :]
    w4 = s4_ref[HALO:n_ext, :]
    w8 = s8_ref[HALO:n_ext, :]
    w16 = w8 + s8_ref[HALO - 8:n_ext - 8, :]
    lane = lax.broadcasted_iota(jnp.int32, (rows, POOL_DIM), 1)
    group = lane // HEAD_DIM
    wsum = jnp.where(group == 0, w2, jnp.where(group == 1, w4, jnp.where(group == 2, w8, w16)))
    window = jnp.where(group == 0, 2, jnp.where(group == 1, 4, jnp.where(group == 2, 8, 16)))
    pos = step * rows + lax.broadcasted_iota(jnp.int32, (rows, POOL_DIM), 0)
    count = jnp.minimum(pos + 1, window).astype(_F32)
    pooled = wsum / count - c_in
    c = _dot(pooled.astype(_BF16), pw_ref[...]) * ps_ref[...]
    pool_ext[0:HALO, :] = pool_ext[rows:rows + HALO, :]

    mix = jnp.concatenate([a, b, c], axis=-1).astype(_BF16)
    o_ref[...] = x + _dot(mix, wout_ref[...])


def _mixer(x2d, batch, seq, norm_g, w_in, ln_g, ln_b, w_sp, b_sp_t, conv_w, conv_b,
           cln_g, cln_b, pool_w_bd, pool_scale, w_out):
    n, d = x2d.shape
    assert seq % MIX_ROWS == 0 and MIX_ROWS % CHUNK == 0
    steps = seq // MIX_ROWS
    row_spec = pl.BlockSpec((MIX_ROWS, d), lambda b, j: (b * steps + j, 0))
    args = [x2d, norm_g.reshape(1, d), w_in, ln_g.reshape(1, -1), ln_b.reshape(1, -1), w_sp, b_sp_t,
            conv_w, conv_b.reshape(1, -1), cln_g.reshape(1, -1), cln_b.reshape(1, -1),
            pool_w_bd, pool_scale.reshape(1, -1), w_out]
    in_specs = [row_spec] + [_resident(a.shape) for a in args[1:]]
    ext_rows = HALO + MIX_ROWS
    return pl.pallas_call(
        _mixer_kernel,
        out_shape=jax.ShapeDtypeStruct((n, d), x2d.dtype),
        grid=(batch, steps),
        in_specs=in_specs,
        out_specs=row_spec,
        scratch_shapes=[
            pltpu.VMEM((ext_rows, CONV_DIM), _F32),
            pltpu.VMEM((ext_rows, POOL_DIM), _F32),
            pltpu.VMEM((ext_rows, POOL_DIM), _F32),
            pltpu.VMEM((ext_rows, POOL_DIM), _F32),
            pltpu.VMEM((ext_rows, POOL_DIM), _F32),
        ],
        compiler_params=pltpu.CompilerParams(
            dimension_semantics=("arbitrary", "arbitrary"), vmem_limit_bytes=VMEM_LIMIT_BYTES),
        name="mixer",
    )(*args)


def _block_diag(blocks):
    g, r, c = blocks.shape
    out = jnp.zeros((g * r, g * c), blocks.dtype)
    for i in range(g):
        out = lax.dynamic_update_slice(out, blocks[i], (i * r, i * c))
    return out


def kernel(x, ffn1_norm, ffn1_w_gate, ffn1_w_up, ffn1_w_down, mix_norm, w_in, sgu_ln_g, sgu_ln_b, w_spatial, b_spatial, conv_w, conv_b, conv_ln_g, conv_ln_b, pool_w, pool_scale, w_out, ffn2_norm, ffn2_w_gate, ffn2_w_up, ffn2_w_down, final_norm):
    batch, seq, d = x.shape
    depth = ffn1_norm.shape[0]
    h = x.reshape(batch * seq, d)
    bf = lambda w: w.astype(_BF16)
    for l in range(depth):
        h = _ffn(h, ffn1_norm[l], bf(ffn1_w_gate[l]), bf(ffn1_w_up[l]), bf(ffn1_w_down[l]))
        b_sp_t = jnp.repeat(b_spatial[l].T, HEAD_DIM, axis=1)
        h = _mixer(h, batch, seq, mix_norm[l], bf(w_in[l]), sgu_ln_g[l], sgu_ln_b[l], w_spatial[l], b_sp_t,
                   conv_w[l], conv_b[l], conv_ln_g[l], conv_ln_b[l], bf(_block_diag(pool_w[l])),
                   pool_scale[l], bf(w_out[l]))
        h = _ffn(h, ffn2_norm[l], bf(ffn2_w_gate[l]), bf(ffn2_w_up[l]), bf(ffn2_w_down[l]),
                 final_g=final_norm if l == depth - 1 else None)
    return h.reshape(batch, seq, d)
```

```python
import functools
import math

import jax
import jax.numpy as jnp
from jax import lax
from jax.experimental import pallas as pl
from jax.experimental.pallas import tpu as pltpu

D_MODEL = 1024
D_FF = 2816
HEAD_DIM = 64
SGU_DIM = 384
SGU_HEADS = SGU_DIM // HEAD_DIM
CONV_DIM = 384
POOL_DIM = 256
POOL_WINDOWS = (2, 4, 8, 16)
CHUNK = 128
CONV_WIDTH = 31
D_IN = 2 * SGU_DIM + 2 * CONV_DIM + POOL_DIM
EPS = 1e-6

LANES = 128
SUBLANES = 8
HALO = 32
HIST_STRIDE = 2
CONV_BLOCK = 32
VMEM_LIMIT_BYTES = 56 * 1024 * 1024

FFN_ROWS = 512
MIX_ROWS = 256

_F32 = jnp.float32
_BF16 = jnp.bfloat16


def _rms_norm(x, g):
    y = x * lax.rsqrt(jnp.mean(x * x, axis=-1, keepdims=True) + EPS)
    return y * g


def _layer_norm(x, g, b):
    mu = jnp.mean(x, axis=-1, keepdims=True)
    xc = x - mu
    var = jnp.mean(xc * xc, axis=-1, keepdims=True)
    return xc * lax.rsqrt(var + EPS) * g + b


def _gelu(x):
    return 0.5 * x * (1.0 + lax.erf(x * math.sqrt(0.5)))


def _silu(x):
    return x * jax.nn.sigmoid(x)


def _dot(a, b):
    return jnp.dot(a, b, preferred_element_type=_F32)


def _ffn_kernel(x_ref, g_ref, wg_ref, wu_ref, wd_ref, *rest, final_norm):
    if final_norm:
        fg_ref, o_ref = rest
    else:
        (o_ref,) = rest
    x = x_ref[...]
    h = _rms_norm(x, g_ref[...]).astype(_BF16)
    gate = _dot(h, wg_ref[...])
    up = _dot(h, wu_ref[...])
    act = (_silu(gate) * up).astype(_BF16)
    y = x + 0.5 * _dot(act, wd_ref[...])
    if final_norm:
        y = _rms_norm(y, fg_ref[...])
    o_ref[...] = y


def _resident(shape):
    return pl.BlockSpec(shape, lambda *_: (0,) * len(shape), pipeline_mode=pl.Buffered(1))


def _ffn(x2d, norm_g, w_gate, w_up, w_down, final_g=None):
    n, d = x2d.shape
    f = w_gate.shape[1]
    assert n % FFN_ROWS == 0
    row_spec = pl.BlockSpec((FFN_ROWS, d), lambda i: (i, 0))
    in_specs = [row_spec, _resident((1, d)), _resident((d, f)), _resident((d, f)), _resident((f, d))]
    args = [x2d, norm_g.reshape(1, d), w_gate, w_up, w_down]
    if final_g is not None:
        in_specs.append(_resident((1, d)))
        args.append(final_g.reshape(1, d))
    return pl.pallas_call(
        functools.partial(_ffn_kernel, final_norm=final_g is not None),
        out_shape=jax.ShapeDtypeStruct((n, d), x2d.dtype),
        grid=(n // FFN_ROWS,),
        in_specs=in_specs,
        out_specs=row_spec,
        compiler_params=pltpu.CompilerParams(
            dimension_semantics=("arbitrary",), vmem_limit_bytes=VMEM_LIMIT_BYTES),
        name="ffn_final" if final_g is not None else "ffn",
    )(*args)


def _hist_shape(rows, width):
    return (width // LANES, HIST_STRIDE * (HALO + rows), LANES)


def _hist_window(ref, slab, row):
    return ref[slab, pl.ds(HIST_STRIDE * row, SUBLANES, stride=HIST_STRIDE), :]


def _hist_store(ref, slab, row, val):
    ref[slab, pl.ds(HIST_STRIDE * row, SUBLANES, stride=HIST_STRIDE), :] = val


def _hist_zero_halo(ref):
    zero = jnp.zeros((SUBLANES, LANES), _F32)
    for slab in range(ref.shape[0]):
        for r in range(0, HALO, SUBLANES):
            _hist_store(ref, slab, r, zero)


def _hist_append(ref, val):
    for slab in range(ref.shape[0]):
        for r in range(0, val.shape[0], SUBLANES):
            _hist_store(ref, slab, HALO + r, val[r:r + SUBLANES, slab * LANES:(slab + 1) * LANES])


def _hist_carry(ref, rows):
    for slab in range(ref.shape[0]):
        for r in range(0, HALO, SUBLANES):
            _hist_store(ref, slab, r, _hist_window(ref, slab, rows + r))


def _mix_features(z_ref, seq_tile, lng_ref, lnb_ref, ws_ref, bs_ref, cw_ref, cb_ref, clng_ref, clnb_ref,
                  pw_ref, ps_ref, conv_ext, pool_ext):
    rows = z_ref.shape[0]
    n_chunks = rows // CHUNK
    step = seq_tile
    s1, s2, s3, s4 = SGU_DIM, 2 * SGU_DIM, 2 * SGU_DIM + CONV_DIM, 2 * SGU_DIM + 2 * CONV_DIM
    a_u, a_v = z_ref[:, :s1], z_ref[:, s1:s2]
    b_val, b_gate, c_in = z_ref[:, s2:s3], z_ref[:, s3:s4], z_ref[:, s4:]

    u = _gelu(a_u)
    v = _layer_norm(_gelu(a_v), lng_ref[...], lnb_ref[...])
    causal = (lax.broadcasted_iota(jnp.int32, (CHUNK, CHUNK), 0)
              >= lax.broadcasted_iota(jnp.int32, (CHUNK, CHUNK), 1))
    left = lax.broadcasted_iota(jnp.int32, (rows, LANES), 1) < HEAD_DIM
    bias = bs_ref[...]
    mixed_cols = []
    for pair in range(SGU_HEADS // 2):
        vp = v[:, pair * LANES:(pair + 1) * LANES]
        v_l = jnp.where(left, vp, 0.0).astype(_BF16)
        v_r = jnp.where(left, 0.0, vp).astype(_BF16)
        rhs = jnp.concatenate(
            [jnp.concatenate([v_l[n * CHUNK:(n + 1) * CHUNK], v_r[n * CHUNK:(n + 1) * CHUNK]], axis=0)
             for n in range(n_chunks)], axis=1)
        lhs = jnp.concatenate(
            [jnp.where(causal, ws_ref[2 * pair], 0.0), jnp.where(causal, ws_ref[2 * pair + 1], 0.0)],
            axis=1).astype(_BF16)
        out = _dot(lhs, rhs)
        mixed_cols.append(jnp.concatenate(
            [out[:, n * CHUNK:(n + 1) * CHUNK] for n in range(n_chunks)], axis=0))
    mixed = jnp.concatenate(mixed_cols, axis=1) + jnp.concatenate([bias] * n_chunks, axis=0)
    a = u * mixed

    _hist_append(conv_ext, b_val * jax.nn.sigmoid(b_gate))
    first = HALO - (CONV_WIDTH - 1)
    n_sub = CONV_BLOCK // SUBLANES
    b_blocks = []
    for t0 in range(0, rows, CONV_BLOCK):
        acc = [[None] * n_sub for _ in range(CONV_DIM // LANES)]
        for k in range(CONV_WIDTH):
            for s in range(CONV_DIM // LANES):
                w_k = cw_ref[k:k + 1, s * LANES:(s + 1) * LANES]
                for i in range(n_sub):
                    term = _hist_window(conv_ext, s, t0 + i * SUBLANES + first + k) * w_k
                    acc[s][i] = term if acc[s][i] is None else acc[s][i] + term
        conv = jnp.concatenate([jnp.concatenate(col, axis=0) for col in acc], axis=1) + cb_ref[...]
        b_blocks.append(_silu(_layer_norm(conv, clng_ref[...], clnb_ref[...])))
    b = jnp.concatenate(b_blocks, axis=0)
    _hist_carry(conv_ext, rows)

    _hist_append(pool_ext, c_in)
    left8 = lax.broadcasted_iota(jnp.int32, (SUBLANES, LANES), 1) < HEAD_DIM
    row8 = lax.broadcasted_iota(jnp.int32, (SUBLANES, LANES), 0)
    windows = [jnp.where(left8, POOL_WINDOWS[2 * s], POOL_WINDOWS[2 * s + 1])
               for s in range(POOL_DIM // LANES)]
    pooled_rows = []
    for r in range(0, rows, SUBLANES):
        pos1 = step * rows + (r + 1) + row8
        cols = []
        for s in range(POOL_DIM // LANES):
            narrow = POOL_WINDOWS[2 * s]
            cur = c_in[r:r + SUBLANES, s * LANES:(s + 1) * LANES]
            run = cur
            for j in range(1, POOL_WINDOWS[2 * s + 1]):
                run = run + _hist_window(pool_ext, s, HALO + r - j)
                if j == narrow - 1:
                    run_narrow = run
            wsum = jnp.where(left8, run_narrow, run)
            count = jnp.minimum(pos1, windows[s]).astype(_F32)
            cols.append(wsum / count - cur)
        pooled_rows.append(jnp.concatenate(cols, axis=1))
    pooled = jnp.concatenate(pooled_rows, axis=0)
    c = _dot(pooled.astype(_BF16), pw_ref[...]) * ps_ref[...]
    _hist_carry(pool_ext, rows)

    return jnp.concatenate([a, b, c], axis=-1)


def _mixer_kernel(x_ref, xres_ref, g_ref, win_ref, lng_ref, lnb_ref, ws_ref, bs_ref,
                  cw_ref, cb_ref, clng_ref, clnb_ref, pw_ref, ps_ref, wout_ref,
                  o_ref, z_even, z_odd, mix_even, mix_odd, conv_ext, pool_ext, *, tiles_per_seq):
    step = pl.program_id(0)
    seq_tile = lax.rem(step + (tiles_per_seq - 1), tiles_per_seq)

    @pl.when(step == 0)
    def _():
        z_odd[...] = jnp.zeros(z_odd.shape, z_odd.dtype)
        mix_odd[...] = jnp.zeros(mix_odd.shape, mix_odd.dtype)

    @pl.when(jnp.logical_or(seq_tile == 0, step == 0))
    def _():
        _hist_zero_halo(conv_ext)
        _hist_zero_halo(pool_ext)

    def stages(z_write, z_read, mix_write, mix_read):
        hn = _rms_norm(x_ref[...], g_ref[...]).astype(_BF16)
        z_write[...] = _dot(hn, win_ref[...])
        mix_write[...] = _mix_features(z_read, seq_tile, lng_ref, lnb_ref, ws_ref, bs_ref, cw_ref, cb_ref,
                                       clng_ref, clnb_ref, pw_ref, ps_ref, conv_ext, pool_ext).astype(_BF16)
        o_ref[...] = xres_ref[...] + _dot(mix_read[...], wout_ref[...])

    even = lax.rem(step, 2) == 0

    @pl.when(even)
    def _():
        stages(z_even, z_odd, mix_even, mix_odd)

    @pl.when(jnp.logical_not(even))
    def _():
        stages(z_odd, z_even, mix_odd, mix_even)


def _mixer(x2d, seq, norm_g, w_in, ln_g, ln_b, w_sp, b_sp_t, conv_w, conv_b,
           cln_g, cln_b, pool_w_bd, pool_scale, w_out):
    n, d = x2d.shape
    assert seq % MIX_ROWS == 0 and MIX_ROWS % CHUNK == 0 and n % seq == 0
    tiles = n // MIX_ROWS
    depth = 2
    lead_spec = pl.BlockSpec((MIX_ROWS, d), lambda s: (jnp.minimum(s, tiles - 1), 0))
    lag_spec = pl.BlockSpec((MIX_ROWS, d), lambda s: (jnp.maximum(s - depth, 0), 0))
    args = [x2d, x2d, norm_g.reshape(1, d), w_in, ln_g.reshape(1, -1), ln_b.reshape(1, -1), w_sp, b_sp_t,
            conv_w, conv_b.reshape(1, -1), cln_g.reshape(1, -1), cln_b.reshape(1, -1),
            pool_w_bd, pool_scale.reshape(1, -1), w_out]
    in_specs = [lead_spec, lag_spec] + [_resident(a.shape) for a in args[2:]]
    return pl.pallas_call(
        functools.partial(_mixer_kernel, tiles_per_seq=seq // MIX_ROWS),
        out_shape=jax.ShapeDtypeStruct((n, d), x2d.dtype),
        grid=(tiles + depth,),
        in_specs=in_specs,
        out_specs=lag_spec,
        scratch_shapes=[
            pltpu.VMEM((MIX_ROWS, D_IN), _F32),
            pltpu.VMEM((MIX_ROWS, D_IN), _F32),
            pltpu.VMEM((MIX_ROWS, d), _BF16),
            pltpu.VMEM((MIX_ROWS, d), _BF16),
            pltpu.VMEM(_hist_shape(MIX_ROWS, CONV_DIM), _F32),
            pltpu.VMEM(_hist_shape(MIX_ROWS, POOL_DIM), _F32),
        ],
        compiler_params=pltpu.CompilerParams(
            dimension_semantics=("arbitrary",), vmem_limit_bytes=VMEM_LIMIT_BYTES),
        name="mixer",
    )(*args)


def _block_diag(blocks):
    g, r, c = blocks.shape
    out = jnp.zeros((g * r, g * c), blocks.dtype)
    for i in range(g):
        out = lax.dynamic_update_slice(out, blocks[i], (i * r, i * c))
    return out


def kernel(x, ffn1_norm, ffn1_w_gate, ffn1_w_up, ffn1_w_down, mix_norm, w_in, sgu_ln_g, sgu_ln_b, w_spatial, b_spatial, conv_w, conv_b, conv_ln_g, conv_ln_b, pool_w, pool_scale, w_out, ffn2_norm, ffn2_w_gate, ffn2_w_up, ffn2_w_down, final_norm):
    batch, seq, d = x.shape
    depth = ffn1_norm.shape[0]
    h = x.reshape(batch * seq, d)
    bf = lambda w: w.astype(_BF16)
    for l in range(depth):
        h = _ffn(h, ffn1_norm[l], bf(ffn1_w_gate[l]), bf(ffn1_w_up[l]), bf(ffn1_w_down[l]))
        b_sp_t = jnp.repeat(b_spatial[l].T, HEAD_DIM, axis=1)
        h = _mixer(h, seq, mix_norm[l], bf(w_in[l]), sgu_ln_g[l], sgu_ln_b[l], w_spatial[l], b_sp_t,
                   conv_w[l], conv_b[l], conv_ln_g[l], conv_ln_b[l], bf(_block_diag(pool_w[l])),
                   pool_scale[l], bf(w_out[l]))
        h = _ffn(h, ffn2_norm[l], bf(ffn2_w_gate[l]), bf(ffn2_w_up[l]), bf(ffn2_w_down[l]),
                 final_g=final_norm if l == depth - 1 else None)
    return h.reshape(batch, seq, d)
```

```python
import functools
import math

import jax
import jax.numpy as jnp
from jax import lax
from jax.experimental import pallas as pl
from jax.experimental.pallas import tpu as pltpu

D_MODEL = 1024
D_FF = 2816
HEAD_DIM = 64
SGU_DIM = 384
SGU_HEADS = SGU_DIM // HEAD_DIM
CONV_DIM = 384
POOL_DIM = 256
POOL_WINDOWS = (2, 4, 8, 16)
CHUNK = 128
CONV_WIDTH = 31
D_IN = 2 * SGU_DIM + 2 * CONV_DIM + POOL_DIM
EPS = 1e-6

LANES = 128
SUBLANES = 8
BF16_ROWS = 16
HALO = 32
HIST_STRIDE = 2
CONV_BLOCK = 32
VMEM_LIMIT_BYTES = 56 * 1024 * 1024

FFN_ROWS = 512
MIX_ROWS = 256
CAST_STEPS = 32

_F32 = jnp.float32
_BF16 = jnp.bfloat16


def _rms_norm(x, g):
    y = x * lax.rsqrt(jnp.mean(x * x, axis=-1, keepdims=True) + EPS)
    return y * g


def _layer_norm(x, g, b):
    mu = jnp.mean(x, axis=-1, keepdims=True)
    xc = x - mu
    var = jnp.mean(xc * xc, axis=-1, keepdims=True)
    return xc * lax.rsqrt(var + EPS) * g + b


def _gelu(x):
    return 0.5 * x * (1.0 + lax.erf(x * math.sqrt(0.5)))


def _silu(x):
    return x * jax.nn.sigmoid(x)


def _dot(a, b):
    return jnp.dot(a, b, preferred_element_type=_F32)


def _ffn_kernel(x_ref, g_ref, wg_ref, wu_ref, wd_ref, *rest, final_norm, n_casts):
    rest = list(rest)
    fg_ref = rest.pop(0) if final_norm else None
    cast_srcs, o_ref, cast_dsts = rest[:n_casts], rest[n_casts], rest[n_casts + 1:]
    _cast_blocks(cast_srcs, cast_dsts)
    x = x_ref[...]
    h = _rms_norm(x, g_ref[...]).astype(_BF16)
    gate = _dot(h, wg_ref[...])
    up = _dot(h, wu_ref[...])
    act = (_silu(gate) * up).astype(_BF16)
    y = x + 0.5 * _dot(act, wd_ref[...])
    if final_norm:
        y = _rms_norm(y, fg_ref[...])
    o_ref[...] = y


def _resident(shape):
    return pl.BlockSpec(shape, lambda *_: (0,) * len(shape), pipeline_mode=pl.Buffered(1))


def _cast_blocks(src_refs, dst_refs):
    for src, dst in zip(src_refs, dst_refs):
        dst[...] = src[...].astype(dst.dtype)


def _cast_plan(stacked, layer, steps):
    _, r, c = stacked.shape
    n_blocks = max(k for k in range(1, steps + 1) if r % (k * BF16_ROWS) == 0)
    rows = r // n_blocks
    in_spec = pl.BlockSpec((None, rows, c), lambda i: (layer, jnp.minimum(i, n_blocks - 1), 0))
    out_spec = pl.BlockSpec((rows, c), lambda i: (jnp.minimum(i, n_blocks - 1), 0))
    return stacked, in_spec, jax.ShapeDtypeStruct((r, c), _BF16), out_spec


def _cast_kernel(*refs):
    _cast_blocks(refs[:len(refs) // 2], refs[len(refs) // 2:])


def _cast_weights(stacked_list, layer):
    plans = [_cast_plan(w, layer, CAST_STEPS) for w in stacked_list]
    return pl.pallas_call(
        _cast_kernel,
        out_shape=[p[2] for p in plans],
        grid=(CAST_STEPS,),
        in_specs=[p[1] for p in plans],
        out_specs=[p[3] for p in plans],
        compiler_params=pltpu.CompilerParams(dimension_semantics=("arbitrary",)),
        name="cast_weights",
    )(*[p[0] for p in plans])


def _ffn(x2d, norm_g, w_gate, w_up, w_down, final_g=None, cast_next=()):
    n, d = x2d.shape
    f = w_gate.shape[1]
    assert n % FFN_ROWS == 0
    steps = n // FFN_ROWS
    row_spec = pl.BlockSpec((FFN_ROWS, d), lambda i: (i, 0))
    in_specs = [row_spec, _resident((1, d)), _resident((d, f)), _resident((d, f)), _resident((f, d))]
    args = [x2d, norm_g.reshape(1, d), w_gate, w_up, w_down]
    if final_g is not None:
        in_specs.append(_resident((1, d)))
        args.append(final_g.reshape(1, d))
    plans = [_cast_plan(w, layer, steps) for w, layer in cast_next]
    outs = pl.pallas_call(
        functools.partial(_ffn_kernel, final_norm=final_g is not None, n_casts=len(plans)),
        out_shape=[jax.ShapeDtypeStruct((n, d), x2d.dtype)] + [p[2] for p in plans],
        grid=(steps,),
        in_specs=in_specs + [p[1] for p in plans],
        out_specs=[row_spec] + [p[3] for p in plans],
        compiler_params=pltpu.CompilerParams(
            dimension_semantics=("arbitrary",), vmem_limit_bytes=VMEM_LIMIT_BYTES),
        name="ffn_final" if final_g is not None else "ffn",
    )(*args, *[p[0] for p in plans])
    return outs[0], outs[1:]


def _hist_shape(rows, width):
    return (width // LANES, HIST_STRIDE * (HALO + rows), LANES)


def _hist_window(ref, slab, row):
    return ref[slab, pl.ds(HIST_STRIDE * row, SUBLANES, stride=HIST_STRIDE), :]


def _hist_store(ref, slab, row, val):
    ref[slab, pl.ds(HIST_STRIDE * row, SUBLANES, stride=HIST_STRIDE), :] = val


def _hist_zero_halo(ref):
    zero = jnp.zeros((SUBLANES, LANES), _F32)
    for slab in range(ref.shape[0]):
        for r in range(0, HALO, SUBLANES):
            _hist_store(ref, slab, r, zero)


def _hist_append(ref, val):
    for slab in range(ref.shape[0]):
        for r in range(0, val.shape[0], SUBLANES):
            _hist_store(ref, slab, HALO + r, val[r:r + SUBLANES, slab * LANES:(slab + 1) * LANES])


def _hist_carry(ref, rows):
    for slab in range(ref.shape[0]):
        for r in range(0, HALO, SUBLANES):
            _hist_store(ref, slab, r, _hist_window(ref, slab, rows + r))


def _mix_features(z_ref, seq_tile, lng_ref, lnb_ref, ws_ref, bs_ref, cw_ref, cb_ref, clng_ref, clnb_ref,
                  pw_ref, ps_ref, conv_ext, pool_ext):
    rows = z_ref.shape[0]
    n_chunks = rows // CHUNK
    step = seq_tile
    s1, s2, s3, s4 = SGU_DIM, 2 * SGU_DIM, 2 * SGU_DIM + CONV_DIM, 2 * SGU_DIM + 2 * CONV_DIM
    a_u, a_v = z_ref[:, :s1], z_ref[:, s1:s2]
    b_val, b_gate, c_in = z_ref[:, s2:s3], z_ref[:, s3:s4], z_ref[:, s4:]

    u = _gelu(a_u)
    v = _layer_norm(_gelu(a_v), lng_ref[...], lnb_ref[...])
    causal = (lax.broadcasted_iota(jnp.int32, (CHUNK, CHUNK), 0)
              >= lax.broadcasted_iota(jnp.int32, (CHUNK, CHUNK), 1))
    left = lax.broadcasted_iota(jnp.int32, (rows, LANES), 1) < HEAD_DIM
    bias = bs_ref[...]
    mixed_cols = []
    for pair in range(SGU_HEADS // 2):
        vp = v[:, pair * LANES:(pair + 1) * LANES]
        v_l = jnp.where(left, vp, 0.0).astype(_BF16)
        v_r = jnp.where(left, 0.0, vp).astype(_BF16)
        rhs = jnp.concatenate(
            [jnp.concatenate([v_l[n * CHUNK:(n + 1) * CHUNK], v_r[n * CHUNK:(n + 1) * CHUNK]], axis=0)
             for n in range(n_chunks)], axis=1)
        lhs = jnp.concatenate(
            [jnp.where(causal, ws_ref[2 * pair], 0.0), jnp.where(causal, ws_ref[2 * pair + 1], 0.0)],
            axis=1).astype(_BF16)
        out = _dot(lhs, rhs)
        mixed_cols.append(jnp.concatenate(
            [out[:, n * CHUNK:(n + 1) * CHUNK] for n in range(n_chunks)], axis=0))
    mixed = jnp.concatenate(mixed_cols, axis=1) + jnp.concatenate([bias] * n_chunks, axis=0)
    a = u * mixed

    _hist_append(conv_ext, b_val * jax.nn.sigmoid(b_gate))
    first = HALO - (CONV_WIDTH - 1)
    n_sub = CONV_BLOCK // SUBLANES
    b_blocks = []
    for t0 in range(0, rows, CONV_BLOCK):
        acc = [[None] * n_sub for _ in range(CONV_DIM // LANES)]
        for k in range(CONV_WIDTH):
            for s in range(CONV_DIM // LANES):
                w_k = cw_ref[k:k + 1, s * LANES:(s + 1) * LANES]
                for i in range(n_sub):
                    term = _hist_window(conv_ext, s, t0 + i * SUBLANES + first + k) * w_k
                    acc[s][i] = term if acc[s][i] is None else acc[s][i] + term
        conv = jnp.concatenate([jnp.concatenate(col, axis=0) for col in acc], axis=1) + cb_ref[...]
        b_blocks.append(_silu(_layer_norm(conv, clng_ref[...], clnb_ref[...])))
    b = jnp.concatenate(b_blocks, axis=0)
    _hist_carry(conv_ext, rows)

    _hist_append(pool_ext, c_in)
    left8 = lax.broadcasted_iota(jnp.int32, (SUBLANES, LANES), 1) < HEAD_DIM
    row8 = lax.broadcasted_iota(jnp.int32, (SUBLANES, LANES), 0)
    windows = [jnp.where(left8, POOL_WINDOWS[2 * s], POOL_WINDOWS[2 * s + 1])
               for s in range(POOL_DIM // LANES)]
    pooled_rows = []
    for r in range(0, rows, SUBLANES):
        pos1 = step * rows + (r + 1) + row8
        cols = []
        for s in range(POOL_DIM // LANES):
            narrow = POOL_WINDOWS[2 * s]
            cur = c_in[r:r + SUBLANES, s * LANES:(s + 1) * LANES]
            run = cur
            for j in range(1, POOL_WINDOWS[2 * s + 1]):
                run = run + _hist_window(pool_ext, s, HALO + r - j)
                if j == narrow - 1:
                    run_narrow = run
            wsum = jnp.where(left8, run_narrow, run)
            count = jnp.minimum(pos1, windows[s]).astype(_F32)
            cols.append(wsum / count - cur)
        pooled_rows.append(jnp.concatenate(cols, axis=1))
    pooled = jnp.concatenate(pooled_rows, axis=0)
    c = _dot(pooled.astype(_BF16), pw_ref[...]) * ps_ref[...]
    _hist_carry(pool_ext, rows)

    return jnp.concatenate([a, b, c], axis=-1)


def _mixer_kernel(x_ref, xres_ref, g_ref, win_ref, lng_ref, lnb_ref, ws_ref, bs_ref,
                  cw_ref, cb_ref, clng_ref, clnb_ref, pw_ref, ps_ref, wout_ref,
                  o_ref, z_even, z_odd, mix_even, mix_odd, conv_ext, pool_ext, *, tiles_per_seq):
    step = pl.program_id(0)
    seq_tile = lax.rem(step + (tiles_per_seq - 1), tiles_per_seq)

    @pl.when(step == 0)
    def _():
        z_odd[...] = jnp.zeros(z_odd.shape, z_odd.dtype)
        mix_odd[...] = jnp.zeros(mix_odd.shape, mix_odd.dtype)

    @pl.when(jnp.logical_or(seq_tile == 0, step == 0))
    def _():
        _hist_zero_halo(conv_ext)
        _hist_zero_halo(pool_ext)

    def stages(z_write, z_read, mix_write, mix_read):
        hn = _rms_norm(x_ref[...], g_ref[...]).astype(_BF16)
        z_write[...] = _dot(hn, win_ref[...])
        mix_write[...] = _mix_features(z_read, seq_tile, lng_ref, lnb_ref, ws_ref, bs_ref, cw_ref, cb_ref,
                                       clng_ref, clnb_ref, pw_ref, ps_ref, conv_ext, pool_ext).astype(_BF16)
        o_ref[...] = xres_ref[...] + _dot(mix_read[...], wout_ref[...])

    even = lax.rem(step, 2) == 0

    @pl.when(even)
    def _():
        stages(z_even, z_odd, mix_even, mix_odd)

    @pl.when(jnp.logical_not(even))
    def _():
        stages(z_odd, z_even, mix_odd, mix_even)


def _mixer(x2d, seq, norm_g, w_in, ln_g, ln_b, w_sp, b_sp_t, conv_w, conv_b,
           cln_g, cln_b, pool_w_bd, pool_scale, w_out):
    n, d = x2d.shape
    assert seq % MIX_ROWS == 0 and MIX_ROWS % CHUNK == 0 and n % seq == 0
    tiles = n // MIX_ROWS
    depth = 2
    lead_spec = pl.BlockSpec((MIX_ROWS, d), lambda s: (jnp.minimum(s, tiles - 1), 0))
    lag_spec = pl.BlockSpec((MIX_ROWS, d), lambda s: (jnp.maximum(s - depth, 0), 0))
    args = [x2d, x2d, norm_g.reshape(1, d), w_in, ln_g.reshape(1, -1), ln_b.reshape(1, -1), w_sp, b_sp_t,
            conv_w, conv_b.reshape(1, -1), cln_g.reshape(1, -1), cln_b.reshape(1, -1),
            pool_w_bd, pool_scale.reshape(1, -1), w_out]
    in_specs = [lead_spec, lag_spec] + [_resident(a.shape) for a in args[2:]]
    return pl.pallas_call(
        functools.partial(_mixer_kernel, tiles_per_seq=seq // MIX_ROWS),
        out_shape=jax.ShapeDtypeStruct((n, d), x2d.dtype),
        grid=(tiles + depth,),
        in_specs=in_specs,
        out_specs=lag_spec,
        scratch_shapes=[
            pltpu.VMEM((MIX_ROWS, D_IN), _F32),
            pltpu.VMEM((MIX_ROWS, D_IN), _F32),
            pltpu.VMEM((MIX_ROWS, d), _BF16),
            pltpu.VMEM((MIX_ROWS, d), _BF16),
            pltpu.VMEM(_hist_shape(MIX_ROWS, CONV_DIM), _F32),
            pltpu.VMEM(_hist_shape(MIX_ROWS, POOL_DIM), _F32),
        ],
        compiler_params=pltpu.CompilerParams(
            dimension_semantics=("arbitrary",), vmem_limit_bytes=VMEM_LIMIT_BYTES),
        name="mixer",
    )(*args)


def _block_diag(blocks):
    g, r, c = blocks.shape
    out = jnp.zeros((g * r, g * c), blocks.dtype)
    for i in range(g):
        out = lax.dynamic_update_slice(out, blocks[i], (i * r, i * c))
    return out


def kernel(x, ffn1_norm, ffn1_w_gate, ffn1_w_up, ffn1_w_down, mix_norm, w_in, sgu_ln_g, sgu_ln_b, w_spatial, b_spatial, conv_w, conv_b, conv_ln_g, conv_ln_b, pool_w, pool_scale, w_out, ffn2_norm, ffn2_w_gate, ffn2_w_up, ffn2_w_down, final_norm):
    batch, seq, d = x.shape
    depth = ffn1_norm.shape[0]
    h = x.reshape(batch * seq, d)
    ffn1_w = _cast_weights([ffn1_w_gate, ffn1_w_up, ffn1_w_down], 0)
    for l in range(depth):
        later = [(w_in, l), (w_out, l), (ffn2_w_gate, l), (ffn2_w_up, l), (ffn2_w_down, l)]
        h, (w_in_l, w_out_l, *ffn2_w) = _ffn(h, ffn1_norm[l], *ffn1_w, cast_next=later)
        b_sp_t = jnp.repeat(b_spatial[l].T, HEAD_DIM, axis=1)
        h = _mixer(h, seq, mix_norm[l], w_in_l, sgu_ln_g[l], sgu_ln_b[l], w_spatial[l], b_sp_t,
                   conv_w[l], conv_b[l], conv_ln_g[l], conv_ln_b[l], _block_diag(pool_w[l]).astype(_BF16),
                   pool_scale[l], w_out_l)
        last = l == depth - 1
        later = [] if last else [(ffn1_w_gate, l + 1), (ffn1_w_up, l + 1), (ffn1_w_down, l + 1)]
        h, ffn1_w = _ffn(h, ffn2_norm[l], *ffn2_w, final_g=final_norm if last else None, cast_next=later)
    return h.reshape(batch, seq, d)
```

```python
import functools
import math

import jax
import jax.numpy as jnp
from jax import lax
from jax.experimental import pallas as pl
from jax.experimental.pallas import tpu as pltpu

D_MODEL = 1024
D_FF = 2816
HEAD_DIM = 64
SGU_DIM = 384
SGU_HEADS = SGU_DIM // HEAD_DIM
CONV_DIM = 384
POOL_DIM = 256
POOL_WINDOWS = (2, 4, 8, 16)
CHUNK = 128
CONV_WIDTH = 31
D_IN = 2 * SGU_DIM + 2 * CONV_DIM + POOL_DIM
EPS = 1e-6

LANES = 128
SUBLANES = 8
BF16_ROWS = 16
HALO = 32
HIST_STRIDE = 2
CONV_BLOCK = 32
VMEM_LIMIT_BYTES = 56 * 1024 * 1024

FFN_ROWS = 512
MIX_ROWS = 256
CAST_STEPS = 32

_F32 = jnp.float32
_BF16 = jnp.bfloat16


def _rms_norm(x, g):
    y = x * lax.rsqrt(jnp.mean(x * x, axis=-1, keepdims=True) + EPS)
    return y * g


def _layer_norm(x, g, b):
    mu = jnp.mean(x, axis=-1, keepdims=True)
    xc = x - mu
    var = jnp.mean(xc * xc, axis=-1, keepdims=True)
    return xc * lax.rsqrt(var + EPS) * g + b


def _gelu(x):
    return 0.5 * x * (1.0 + lax.erf(x * math.sqrt(0.5)))


def _silu(x):
    return x * jax.nn.sigmoid(x)


def _dot(a, b):
    return jnp.dot(a, b, preferred_element_type=_F32)


def _ffn_tile(x, g_ref, wg_ref, wu_ref, wd_ref, fg_ref):
    h = _rms_norm(x, g_ref[...]).astype(_BF16)
    gate = _dot(h, wg_ref[...])
    up = _dot(h, wu_ref[...])
    act = (_silu(gate) * up).astype(_BF16)
    y = x + 0.5 * _dot(act, wd_ref[...])
    if fg_ref is not None:
        y = _rms_norm(y, fg_ref[...])
    return y


def _ffn_kernel(x_ref, g_ref, wg_ref, wu_ref, wd_ref, *rest, final_norm, n_casts):
    rest = list(rest)
    fg_ref = rest.pop(0) if final_norm else None
    cast_srcs, o_ref, cast_dsts = rest[:n_casts], rest[n_casts], rest[n_casts + 1:]
    _cast_blocks(cast_srcs, cast_dsts)
    o_ref[...] = _ffn_tile(x_ref[...], g_ref, wg_ref, wu_ref, wd_ref, fg_ref)


def _resident(shape):
    return pl.BlockSpec(shape, lambda *_: (0,) * len(shape), pipeline_mode=pl.Buffered(1))


def _cast_blocks(src_refs, dst_refs):
    for src, dst in zip(src_refs, dst_refs):
        dst[...] = src[...].astype(dst.dtype)


def _cast_plan(stacked, layer, steps):
    _, r, c = stacked.shape
    n_blocks = max(k for k in range(1, steps + 1) if r % (k * BF16_ROWS) == 0)
    rows = r // n_blocks
    in_spec = pl.BlockSpec((None, rows, c), lambda i: (layer, jnp.minimum(i, n_blocks - 1), 0))
    out_spec = pl.BlockSpec((rows, c), lambda i: (jnp.minimum(i, n_blocks - 1), 0))
    return stacked, in_spec, jax.ShapeDtypeStruct((r, c), _BF16), out_spec


def _cast_kernel(*refs):
    _cast_blocks(refs[:len(refs) // 2], refs[len(refs) // 2:])


def _cast_weights(stacked_list, layer):
    plans = [_cast_plan(w, layer, CAST_STEPS) for w in stacked_list]
    return pl.pallas_call(
        _cast_kernel,
        out_shape=[p[2] for p in plans],
        grid=(CAST_STEPS,),
        in_specs=[p[1] for p in plans],
        out_specs=[p[3] for p in plans],
        compiler_params=pltpu.CompilerParams(dimension_semantics=("arbitrary",)),
        name="cast_weights",
    )(*[p[0] for p in plans])


def _ffn(x2d, norm_g, w_gate, w_up, w_down, final_g=None, cast_next=()):
    n, d = x2d.shape
    f = w_gate.shape[1]
    assert n % FFN_ROWS == 0
    steps = n // FFN_ROWS
    row_spec = pl.BlockSpec((FFN_ROWS, d), lambda i: (i, 0))
    in_specs = [row_spec, _resident((1, d)), _resident((d, f)), _resident((d, f)), _resident((f, d))]
    args = [x2d, norm_g.reshape(1, d), w_gate, w_up, w_down]
    if final_g is not None:
        in_specs.append(_resident((1, d)))
        args.append(final_g.reshape(1, d))
    plans = [_cast_plan(w, layer, steps) for w, layer in cast_next]
    outs = pl.pallas_call(
        functools.partial(_ffn_kernel, final_norm=final_g is not None, n_casts=len(plans)),
        out_shape=[jax.ShapeDtypeStruct((n, d), x2d.dtype)] + [p[2] for p in plans],
        grid=(steps,),
        in_specs=in_specs + [p[1] for p in plans],
        out_specs=[row_spec] + [p[3] for p in plans],
        compiler_params=pltpu.CompilerParams(
            dimension_semantics=("arbitrary",), vmem_limit_bytes=VMEM_LIMIT_BYTES),
        name="ffn_final" if final_g is not None else "ffn",
    )(*args, *[p[0] for p in plans])
    return outs[0], outs[1:]


def _hist_shape(rows, width):
    return (width // LANES, HIST_STRIDE * (HALO + rows), LANES)


def _hist_window(ref, slab, row):
    return ref[slab, pl.ds(HIST_STRIDE * row, SUBLANES, stride=HIST_STRIDE), :]


def _hist_store(ref, slab, row, val):
    ref[slab, pl.ds(HIST_STRIDE * row, SUBLANES, stride=HIST_STRIDE), :] = val


def _hist_zero_halo(ref):
    zero = jnp.zeros((SUBLANES, LANES), _F32)
    for slab in range(ref.shape[0]):
        for r in range(0, HALO, SUBLANES):
            _hist_store(ref, slab, r, zero)


def _hist_append(ref, val):
    for slab in range(ref.shape[0]):
        for r in range(0, val.shape[0], SUBLANES):
            _hist_store(ref, slab, HALO + r, val[r:r + SUBLANES, slab * LANES:(slab + 1) * LANES])


def _hist_carry(ref, rows):
    for slab in range(ref.shape[0]):
        for r in range(0, HALO, SUBLANES):
            _hist_store(ref, slab, r, _hist_window(ref, slab, rows + r))


def _mix_features(z_ref, seq_tile, lng_ref, lnb_ref, ws_ref, bs_ref, cw_ref, cb_ref, clng_ref, clnb_ref,
                  pw_ref, ps_ref, conv_ext, pool_ext):
    rows = z_ref.shape[0]
    n_chunks = rows // CHUNK
    step = seq_tile
    s1, s2, s3, s4 = SGU_DIM, 2 * SGU_DIM, 2 * SGU_DIM + CONV_DIM, 2 * SGU_DIM + 2 * CONV_DIM
    a_u, a_v = z_ref[:, :s1], z_ref[:, s1:s2]
    b_val, b_gate, c_in = z_ref[:, s2:s3], z_ref[:, s3:s4], z_ref[:, s4:]

    u = _gelu(a_u)
    v = _layer_norm(_gelu(a_v), lng_ref[...], lnb_ref[...])
    causal = (lax.broadcasted_iota(jnp.int32, (CHUNK, CHUNK), 0)
              >= lax.broadcasted_iota(jnp.int32, (CHUNK, CHUNK), 1))
    left = lax.broadcasted_iota(jnp.int32, (rows, LANES), 1) < HEAD_DIM
    bias = bs_ref[...]
    mixed_cols = []
    for pair in range(SGU_HEADS // 2):
        vp = v[:, pair * LANES:(pair + 1) * LANES]
        v_l = jnp.where(left, vp, 0.0).astype(_BF16)
        v_r = jnp.where(left, 0.0, vp).astype(_BF16)
        rhs = jnp.concatenate(
            [jnp.concatenate([v_l[n * CHUNK:(n + 1) * CHUNK], v_r[n * CHUNK:(n + 1) * CHUNK]], axis=0)
             for n in range(n_chunks)], axis=1)
        lhs = jnp.concatenate(
            [jnp.where(causal, ws_ref[2 * pair], 0.0), jnp.where(causal, ws_ref[2 * pair + 1], 0.0)],
            axis=1).astype(_BF16)
        out = _dot(lhs, rhs)
        mixed_cols.append(jnp.concatenate(
            [out[:, n * CHUNK:(n + 1) * CHUNK] for n in range(n_chunks)], axis=0))
    mixed = jnp.concatenate(mixed_cols, axis=1) + jnp.concatenate([bias] * n_chunks, axis=0)
    a = u * mixed

    _hist_append(conv_ext, b_val * jax.nn.sigmoid(b_gate))
    first = HALO - (CONV_WIDTH - 1)
    n_sub = CONV_BLOCK // SUBLANES
    b_blocks = []
    for t0 in range(0, rows, CONV_BLOCK):
        acc = [[None] * n_sub for _ in range(CONV_DIM // LANES)]
        for k in range(CONV_WIDTH):
            for s in range(CONV_DIM // LANES):
                w_k = cw_ref[k:k + 1, s * LANES:(s + 1) * LANES]
                for i in range(n_sub):
                    term = _hist_window(conv_ext, s, t0 + i * SUBLANES + first + k) * w_k
                    acc[s][i] = term if acc[s][i] is None else acc[s][i] + term
        conv = jnp.concatenate([jnp.concatenate(col, axis=0) for col in acc], axis=1) + cb_ref[...]
        b_blocks.append(_silu(_layer_norm(conv, clng_ref[...], clnb_ref[...])))
    b = jnp.concatenate(b_blocks, axis=0)
    _hist_carry(conv_ext, rows)

    _hist_append(pool_ext, c_in)
    left8 = lax.broadcasted_iota(jnp.int32, (SUBLANES, LANES), 1) < HEAD_DIM
    row8 = lax.broadcasted_iota(jnp.int32, (SUBLANES, LANES), 0)
    windows = [jnp.where(left8, POOL_WINDOWS[2 * s], POOL_WINDOWS[2 * s + 1])
               for s in range(POOL_DIM // LANES)]
    pooled_rows = []
    for r in range(0, rows, SUBLANES):
        pos1 = step * rows + (r + 1) + row8
        cols = []
        for s in range(POOL_DIM // LANES):
            narrow = POOL_WINDOWS[2 * s]
            cur = c_in[r:r + SUBLANES, s * LANES:(s + 1) * LANES]
            run = cur
            for j in range(1, POOL_WINDOWS[2 * s + 1]):
                run = run + _hist_window(pool_ext, s, HALO + r - j)
                if j == narrow - 1:
                    run_narrow = run
            wsum = jnp.where(left8, run_narrow, run)
            count = jnp.minimum(pos1, windows[s]).astype(_F32)
            cols.append(wsum / count - cur)
        pooled_rows.append(jnp.concatenate(cols, axis=1))
    pooled = jnp.concatenate(pooled_rows, axis=0)
    c = _dot(pooled.astype(_BF16), pw_ref[...]) * ps_ref[...]
    _hist_carry(pool_ext, rows)

    return jnp.concatenate([a, b, c], axis=-1)


def _mixer_ffn_kernel(x_ref, xres_ref, g_ref, win_ref, lng_ref, lnb_ref, ws_ref, bs_ref,
                      cw_ref, cb_ref, clng_ref, clnb_ref, pw_ref, ps_ref, wout_ref,
                      fg_ref, fwg_ref, fwu_ref, fwd_ref, *rest, tiles_per_seq, final_norm, n_casts):
    rest = list(rest)
    final_ref = rest.pop(0) if final_norm else None
    cast_srcs, o_ref, cast_dsts = rest[:n_casts], rest[n_casts], rest[n_casts + 1:2 * n_casts + 1]
    z_even, z_odd, mix_even, mix_odd, y_even, y_odd, conv_ext, pool_ext = rest[2 * n_casts + 1:]
    step = pl.program_id(0)
    seq_tile = lax.rem(step + (tiles_per_seq - 1), tiles_per_seq)

    @pl.when(step == 0)
    def _():
        z_odd[...] = jnp.zeros(z_odd.shape, z_odd.dtype)
        mix_odd[...] = jnp.zeros(mix_odd.shape, mix_odd.dtype)
        y_odd[...] = jnp.zeros(y_odd.shape, y_odd.dtype)

    @pl.when(jnp.logical_or(seq_tile == 0, step == 0))
    def _():
        _hist_zero_halo(conv_ext)
        _hist_zero_halo(pool_ext)

    _cast_blocks(cast_srcs, cast_dsts)

    def stages(z_write, z_read, mix_write, mix_read, y_write, y_read):
        hn = _rms_norm(x_ref[...], g_ref[...]).astype(_BF16)
        z_write[...] = _dot(hn, win_ref[...])
        mix_write[...] = _mix_features(z_read, seq_tile, lng_ref, lnb_ref, ws_ref, bs_ref, cw_ref, cb_ref,
                                       clng_ref, clnb_ref, pw_ref, ps_ref, conv_ext, pool_ext).astype(_BF16)
        y_write[...] = xres_ref[...] + _dot(mix_read[...], wout_ref[...])
        o_ref[...] = _ffn_tile(y_read[...], fg_ref, fwg_ref, fwu_ref, fwd_ref, final_ref)

    even = lax.rem(step, 2) == 0

    @pl.when(even)
    def _():
        stages(z_even, z_odd, mix_even, mix_odd, y_even, y_odd)

    @pl.when(jnp.logical_not(even))
    def _():
        stages(z_odd, z_even, mix_odd, mix_even, y_odd, y_even)


def _mixer_ffn(x2d, seq, norm_g, w_in, ln_g, ln_b, w_sp, b_sp_t, conv_w, conv_b, cln_g, cln_b,
               pool_w_bd, pool_scale, w_out, ffn_g, w_gate, w_up, w_down, final_g=None, cast_next=()):
    n, d = x2d.shape
    assert seq % MIX_ROWS == 0 and MIX_ROWS % CHUNK == 0 and n % seq == 0
    tiles = n // MIX_ROWS
    lead_spec = pl.BlockSpec((MIX_ROWS, d), lambda s: (jnp.minimum(s, tiles - 1), 0))
    lag2_spec = pl.BlockSpec((MIX_ROWS, d), lambda s: (jnp.clip(s - 2, 0, tiles - 1), 0))
    lag3_spec = pl.BlockSpec((MIX_ROWS, d), lambda s: (jnp.maximum(s - 3, 0), 0))
    args = [x2d, x2d, norm_g.reshape(1, d), w_in, ln_g.reshape(1, -1), ln_b.reshape(1, -1), w_sp, b_sp_t,
            conv_w, conv_b.reshape(1, -1), cln_g.reshape(1, -1), cln_b.reshape(1, -1),
            pool_w_bd, pool_scale.reshape(1, -1), w_out, ffn_g.reshape(1, d), w_gate, w_up, w_down]
    if final_g is not None:
        args.append(final_g.reshape(1, d))
    in_specs = [lead_spec, lag2_spec] + [_resident(a.shape) for a in args[2:]]
    plans = [_cast_plan(w, layer, tiles) for w, layer in cast_next]
    outs = pl.pallas_call(
        functools.partial(_mixer_ffn_kernel, tiles_per_seq=seq // MIX_ROWS,
                          final_norm=final_g is not None, n_casts=len(plans)),
        out_shape=[jax.ShapeDtypeStruct((n, d), x2d.dtype)] + [p[2] for p in plans],
        grid=(tiles + 3,),
        in_specs=in_specs + [p[1] for p in plans],
        out_specs=[lag3_spec] + [p[3] for p in plans],
        scratch_shapes=[
            pltpu.VMEM((MIX_ROWS, D_IN), _F32),
            pltpu.VMEM((MIX_ROWS, D_IN), _F32),
            pltpu.VMEM((MIX_ROWS, d), _BF16),
            pltpu.VMEM((MIX_ROWS, d), _BF16),
            pltpu.VMEM((MIX_ROWS, d), _F32),
            pltpu.VMEM((MIX_ROWS, d), _F32),
            pltpu.VMEM(_hist_shape(MIX_ROWS, CONV_DIM), _F32),
            pltpu.VMEM(_hist_shape(MIX_ROWS, POOL_DIM), _F32),
        ],
        compiler_params=pltpu.CompilerParams(
            dimension_semantics=("arbitrary",), vmem_limit_bytes=VMEM_LIMIT_BYTES),
        name="mixer_ffn_final" if final_g is not None else "mixer_ffn",
    )(*args, *[p[0] for p in plans])
    return outs[0], outs[1:]


def _block_diag(blocks):
    g, r, c = blocks.shape
    out = jnp.zeros((g * r, g * c), blocks.dtype)
    for i in range(g):
        out = lax.dynamic_update_slice(out, blocks[i], (i * r, i * c))
    return out


def kernel(x, ffn1_norm, ffn1_w_gate, ffn1_w_up, ffn1_w_down, mix_norm, w_in, sgu_ln_g, sgu_ln_b, w_spatial, b_spatial, conv_w, conv_b, conv_ln_g, conv_ln_b, pool_w, pool_scale, w_out, ffn2_norm, ffn2_w_gate, ffn2_w_up, ffn2_w_down, final_norm):
    batch, seq, d = x.shape
    depth = ffn1_norm.shape[0]
    h = x.reshape(batch * seq, d)
    ffn1_w = _cast_weights([ffn1_w_gate, ffn1_w_up, ffn1_w_down], 0)
    for l in range(depth):
        later = [(w_in, l), (w_out, l), (ffn2_w_gate, l), (ffn2_w_up, l), (ffn2_w_down, l)]
        h, (w_in_l, w_out_l, *ffn2_w) = _ffn(h, ffn1_norm[l], *ffn1_w, cast_next=later)
        b_sp_t = jnp.repeat(b_spatial[l].T, HEAD_DIM, axis=1)
        last = l == depth - 1
        later = [] if last else [(ffn1_w_gate, l + 1), (ffn1_w_up, l + 1), (ffn1_w_down, l + 1)]
        h, ffn1_w = _mixer_ffn(
            h, seq, mix_norm[l], w_in_l, sgu_ln_g[l], sgu_ln_b[l], w_spatial[l], b_sp_t,
            conv_w[l], conv_b[l], conv_ln_g[l], conv_ln_b[l], _block_diag(pool_w[l]).astype(_BF16),
            pool_scale[l], w_out_l, ffn2_norm[l], *ffn2_w,
            final_g=final_norm if last else None, cast_next=later)
    return h.reshape(batch, seq, d)
```

```python
import functools
import math

import jax
import jax.numpy as jnp
from jax import lax
from jax.experimental import pallas as pl
from jax.experimental.pallas import tpu as pltpu

D_MODEL = 1024
D_FF = 2816
HEAD_DIM = 64
SGU_DIM = 384
SGU_HEADS = SGU_DIM // HEAD_DIM
CONV_DIM = 384
POOL_DIM = 256
POOL_WINDOWS = (2, 4, 8, 16)
assert all(w & (w - 1) == 0 for w in POOL_WINDOWS)
CHUNK = 128
CONV_WIDTH = 31
D_IN = 2 * SGU_DIM + 2 * CONV_DIM + POOL_DIM
EPS = 1e-6

LANES = 128
SUBLANES = 8
BF16_ROWS = 16
HALO = 32
HIST_STRIDE = 2
CONV_BLOCK = 32
VMEM_LIMIT_BYTES = 56 * 1024 * 1024

FFN_ROWS = 1024
MIX_ROWS = 512
CAST_STEPS = 8

_F32 = jnp.float32
_BF16 = jnp.bfloat16


def _rms_norm(x, g):
    y = x * lax.rsqrt(jnp.mean(x * x, axis=-1, keepdims=True) + EPS)
    return y * g


def _layer_norm(x, g, b):
    mu = jnp.mean(x, axis=-1, keepdims=True)
    xc = x - mu
    var = jnp.mean(xc * xc, axis=-1, keepdims=True)
    return xc * lax.rsqrt(var + EPS) * g + b


def _gelu(x):
    return 0.5 * x * (1.0 + lax.erf(x * math.sqrt(0.5)))


def _silu(x):
    return x * jax.nn.sigmoid(x)


def _dot(a, b):
    return jnp.dot(a, b, preferred_element_type=_F32)


def _resident(shape):
    return pl.BlockSpec(shape, lambda *_: (0,) * len(shape), pipeline_mode=pl.Buffered(1))


def _layer_block(stacked, layer):
    if stacked.ndim == 2:
        return stacked, _resident(stacked.shape)
    shape = stacked.shape[1:]
    spec = pl.BlockSpec((None,) + shape, lambda *_: (layer,) + (0,) * len(shape), pipeline_mode=pl.Buffered(1))
    return stacked, spec


def _layer_row(ref, layer):
    return ref.at[pl.ds(layer, 1), :]


def _cast_blocks(src_refs, dst_refs):
    for src, dst in zip(src_refs, dst_refs):
        dst[...] = src[...].astype(dst.dtype)


def _cast_plan(stacked, layer, steps):
    _, r, c = stacked.shape
    n_blocks = max(k for k in range(1, steps + 1) if r % (k * BF16_ROWS) == 0)
    rows = r // n_blocks
    in_spec = pl.BlockSpec((None, rows, c), lambda i: (layer, jnp.minimum(i, n_blocks - 1), 0))
    out_spec = pl.BlockSpec((rows, c), lambda i: (jnp.minimum(i, n_blocks - 1), 0))
    return stacked, in_spec, jax.ShapeDtypeStruct((r, c), _BF16), out_spec


def _cast_kernel(*refs):
    _cast_blocks(refs[:len(refs) // 2], refs[len(refs) // 2:])


def _cast_weights(stacked_list, layer):
    plans = [_cast_plan(w, layer, CAST_STEPS) for w in stacked_list]
    return pl.pallas_call(
        _cast_kernel,
        out_shape=[p[2] for p in plans],
        grid=(CAST_STEPS,),
        in_specs=[p[1] for p in plans],
        out_specs=[p[3] for p in plans],
        compiler_params=pltpu.CompilerParams(dimension_semantics=("arbitrary",)),
        name="cast_weights",
    )(*[p[0] for p in plans])


def _ffn_kernel(x_ref, g_ref, wg_ref, wu_ref, wd_ref, *rest, layer, final_norm, n_casts):
    g_ref = _layer_row(g_ref, layer)
    rest = list(rest)
    fg_ref = rest.pop(0) if final_norm else None
    cast_srcs, o_ref, cast_dsts = rest[:n_casts], rest[n_casts], rest[n_casts + 1:]
    _cast_blocks(cast_srcs, cast_dsts)
    x = x_ref[...]
    h = _rms_norm(x, g_ref[...]).astype(_BF16)
    gate = _dot(h, wg_ref[...])
    up = _dot(h, wu_ref[...])
    act = (_silu(gate) * up).astype(_BF16)
    y = x + 0.5 * _dot(act, wd_ref[...])
    if final_norm:
        y = _rms_norm(y, fg_ref[...])
    o_ref[...] = y


def _ffn(x2d, norm_g, layer, w_gate, w_up, w_down, final_g=None, cast_next=()):
    n, d = x2d.shape
    f = w_gate.shape[1]
    assert n % FFN_ROWS == 0
    steps = n // FFN_ROWS
    row_spec = pl.BlockSpec((FFN_ROWS, d), lambda i: (i, 0))
    norm_g, g_spec = _layer_block(norm_g, layer)
    in_specs = [row_spec, g_spec, _resident((d, f)), _resident((d, f)), _resident((f, d))]
    args = [x2d, norm_g, w_gate, w_up, w_down]
    if final_g is not None:
        in_specs.append(_resident((1, d)))
        args.append(final_g.reshape(1, d))
    plans = [_cast_plan(w, l, steps) for w, l in cast_next]
    outs = pl.pallas_call(
        functools.partial(_ffn_kernel, layer=layer, final_norm=final_g is not None, n_casts=len(plans)),
        out_shape=[jax.ShapeDtypeStruct((n, d), x2d.dtype)] + [p[2] for p in plans],
        grid=(steps,),
        in_specs=in_specs + [p[1] for p in plans],
        out_specs=[row_spec] + [p[3] for p in plans],
        compiler_params=pltpu.CompilerParams(
            dimension_semantics=("arbitrary",), vmem_limit_bytes=VMEM_LIMIT_BYTES),
        name="ffn_final" if final_g is not None else "ffn",
    )(*args, *[p[0] for p in plans])
    return outs[0], outs[1:]


def _hist_shape(rows, width):
    return (width // LANES, HIST_STRIDE * (HALO + rows), LANES)


def _hist_window(ref, slab, row):
    return ref[slab, pl.ds(HIST_STRIDE * row, SUBLANES, stride=HIST_STRIDE), :]


def _hist_store(ref, slab, row, val):
    ref[slab, pl.ds(HIST_STRIDE * row, SUBLANES, stride=HIST_STRIDE), :] = val


def _hist_zero_halo(ref):
    zero = jnp.zeros((SUBLANES, LANES), _F32)
    for slab in range(ref.shape[0]):
        for r in range(0, HALO, SUBLANES):
            _hist_store(ref, slab, r, zero)


def _hist_append(ref, val):
    for slab in range(ref.shape[0]):
        for r in range(0, val.shape[0], SUBLANES):
            _hist_store(ref, slab, HALO + r, val[r:r + SUBLANES, slab * LANES:(slab + 1) * LANES])


def _hist_carry(ref, rows):
    for slab in range(ref.shape[0]):
        for r in range(0, HALO, SUBLANES):
            _hist_store(ref, slab, r, _hist_window(ref, slab, rows + r))


def _mix_features(z_ref, seq_tile, lng_ref, lnb_ref, ws_ref, bs_ref, cw_ref, cb_ref, clng_ref, clnb_ref,
                  pw_ref, ps_ref, conv_ext, pool_ext):
    rows = z_ref.shape[0]
    n_chunks = rows // CHUNK
    s1, s2, s3, s4 = SGU_DIM, 2 * SGU_DIM, 2 * SGU_DIM + CONV_DIM, 2 * SGU_DIM + 2 * CONV_DIM
    a_u, a_v = z_ref[:, :s1], z_ref[:, s1:s2]
    b_val, b_gate, c_in = z_ref[:, s2:s3], z_ref[:, s3:s4], z_ref[:, s4:]

    u = _gelu(a_u)
    v = _layer_norm(_gelu(a_v), lng_ref[...], lnb_ref[...])
    causal = (lax.broadcasted_iota(jnp.int32, (CHUNK, CHUNK), 0)
              >= lax.broadcasted_iota(jnp.int32, (CHUNK, CHUNK), 1))
    left = lax.broadcasted_iota(jnp.int32, (rows, LANES), 1) < HEAD_DIM
    bias = bs_ref[...]
    mixed_cols = []
    for pair in range(SGU_HEADS // 2):
        vp = v[:, pair * LANES:(pair + 1) * LANES]
        v_l = jnp.where(left, vp, 0.0).astype(_BF16)
        v_r = jnp.where(left, 0.0, vp).astype(_BF16)
        rhs = jnp.concatenate(
            [jnp.concatenate([v_l[n * CHUNK:(n + 1) * CHUNK], v_r[n * CHUNK:(n + 1) * CHUNK]], axis=0)
             for n in range(n_chunks)], axis=1)
        lhs = jnp.concatenate(
            [jnp.where(causal, ws_ref[2 * pair], 0.0), jnp.where(causal, ws_ref[2 * pair + 1], 0.0)],
            axis=1).astype(_BF16)
        out = _dot(lhs, rhs)
        mixed_cols.append(jnp.concatenate(
            [out[:, n * CHUNK:(n + 1) * CHUNK] for n in range(n_chunks)], axis=0))
    mixed = jnp.concatenate(mixed_cols, axis=1) + jnp.concatenate([bias] * n_chunks, axis=0)
    a = u * mixed

    _hist_append(conv_ext, b_val * jax.nn.sigmoid(b_gate))
    first = HALO - (CONV_WIDTH - 1)
    n_sub = CONV_BLOCK // SUBLANES
    b_blocks = []
    for t0 in range(0, rows, CONV_BLOCK):
        acc = [[None] * n_sub for _ in range(CONV_DIM // LANES)]
        for k in range(CONV_WIDTH):
            for s in range(CONV_DIM // LANES):
                w_k = cw_ref[k:k + 1, s * LANES:(s + 1) * LANES]
                for i in range(n_sub):
                    term = _hist_window(conv_ext, s, t0 + i * SUBLANES + first + k) * w_k
                    acc[s][i] = term if acc[s][i] is None else acc[s][i] + term
        conv = jnp.concatenate([jnp.concatenate(col, axis=0) for col in acc], axis=1) + cb_ref[...]
        b_blocks.append(_silu(_layer_norm(conv, clng_ref[...], clnb_ref[...])))
    b = jnp.concatenate(b_blocks, axis=0)
    _hist_carry(conv_ext, rows)

    _hist_append(pool_ext, c_in)
    left8 = lax.broadcasted_iota(jnp.int32, (SUBLANES, LANES), 1) < HEAD_DIM
    row8 = lax.broadcasted_iota(jnp.int32, (SUBLANES, LANES), 0)
    windows = [jnp.where(left8, POOL_WINDOWS[2 * s], POOL_WINDOWS[2 * s + 1])
               for s in range(POOL_DIM // LANES)]
    inv_windows = [1.0 / w.astype(_F32) for w in windows]
    pooled_rows = []
    for r in range(0, rows, SUBLANES):
        pos1 = seq_tile * rows + (r + 1) + row8
        cols = []
        for s in range(POOL_DIM // LANES):
            narrow = POOL_WINDOWS[2 * s]
            cur = c_in[r:r + SUBLANES, s * LANES:(s + 1) * LANES]
            run = cur
            for j in range(1, POOL_WINDOWS[2 * s + 1]):
                run = run + _hist_window(pool_ext, s, HALO + r - j)
                if j == narrow - 1:
                    run_narrow = run
            wsum = jnp.where(left8, run_narrow, run)
            if r >= max(POOL_WINDOWS):
                cols.append(wsum * inv_windows[s] - cur)
            else:
                count = jnp.minimum(pos1, windows[s]).astype(_F32)
                cols.append(wsum / count - cur)
        pooled_rows.append(jnp.concatenate(cols, axis=1))
    pooled = jnp.concatenate(pooled_rows, axis=0)
    c = _dot(pooled.astype(_BF16), pw_ref[...]) * ps_ref[...]
    _hist_carry(pool_ext, rows)

    return jnp.concatenate([a, b, c], axis=-1)


def _mixer_kernel(x_ref, xres_ref, g_ref, win_ref, lng_ref, lnb_ref, ws_ref, bs_ref,
                  cw_ref, cb_ref, clng_ref, clnb_ref, pw_ref, ps_ref, wout_ref,
                  o_ref, z_even, z_odd, mix_even, mix_odd, conv_ext, pool_ext, *, layer, tiles, tiles_per_seq):
    g_ref, lng_ref, lnb_ref, cb_ref, clng_ref, clnb_ref, ps_ref = (
        _layer_row(ref, layer) for ref in (g_ref, lng_ref, lnb_ref, cb_ref, clng_ref, clnb_ref, ps_ref))
    step = pl.program_id(0)
    seq_tile = lax.rem(step + (tiles_per_seq - 1), tiles_per_seq)

    @pl.when(seq_tile == 0)
    def _():
        _hist_zero_halo(conv_ext)
        _hist_zero_halo(pool_ext)

    def stages(parity, project=True, features=True, output=True):
        z_write, z_read = (z_even, z_odd) if parity == 0 else (z_odd, z_even)
        mix_write, mix_read = (mix_even, mix_odd) if parity == 0 else (mix_odd, mix_even)
        if project:
            hn = _rms_norm(x_ref[...], g_ref[...]).astype(_BF16)
            z_write[...] = _dot(hn, win_ref[...])
        if features:
            mix_write[...] = _mix_features(z_read, seq_tile, lng_ref, lnb_ref, ws_ref, bs_ref, cw_ref, cb_ref,
                                           clng_ref, clnb_ref, pw_ref, ps_ref, conv_ext, pool_ext).astype(_BF16)
        if output:
            o_ref[...] = xres_ref[...] + _dot(mix_read[...], wout_ref[...])

    steady = jnp.logical_and(step >= 2, step < tiles)
    for parity in (0, 1):
        pl.when(jnp.logical_and(steady, lax.rem(step, 2) == parity))(functools.partial(stages, parity))
    pl.when(step == 0)(functools.partial(stages, 0, features=False, output=False))
    pl.when(step == 1)(functools.partial(stages, 1, output=False))
    pl.when(step == tiles)(functools.partial(stages, tiles % 2, project=False))
    pl.when(step == tiles + 1)(functools.partial(stages, (tiles + 1) % 2, project=False, features=False))


def _mixer(x2d, seq, layer, norm_g, w_in, ln_g, ln_b, w_sp, b_sp_t, conv_w, conv_b,
           cln_g, cln_b, pool_w_bd, pool_scale, w_out):
    n, d = x2d.shape
    assert seq % MIX_ROWS == 0 and MIX_ROWS % CHUNK == 0 and n % seq == 0
    tiles = n // MIX_ROWS
    depth = 2
    lead_spec = pl.BlockSpec((MIX_ROWS, d), lambda s: (jnp.minimum(s, tiles - 1), 0))
    lag_spec = pl.BlockSpec((MIX_ROWS, d), lambda s: (jnp.maximum(s - depth, 0), 0))
    per_layer = lambda p: _layer_block(p, layer)
    params = [per_layer(norm_g), (w_in, _resident(w_in.shape)), per_layer(ln_g), per_layer(ln_b), per_layer(w_sp),
              per_layer(b_sp_t), per_layer(conv_w), per_layer(conv_b), per_layer(cln_g), per_layer(cln_b),
              per_layer(pool_w_bd), per_layer(pool_scale), (w_out, _resident(w_out.shape))]
    return pl.pallas_call(
        functools.partial(_mixer_kernel, layer=layer, tiles=tiles, tiles_per_seq=seq // MIX_ROWS),
        out_shape=jax.ShapeDtypeStruct((n, d), x2d.dtype),
        grid=(tiles + depth,),
        in_specs=[lead_spec, lag_spec] + [spec for _, spec in params],
        out_specs=lag_spec,
        scratch_shapes=[
            pltpu.VMEM((MIX_ROWS, D_IN), _F32),
            pltpu.VMEM((MIX_ROWS, D_IN), _F32),
            pltpu.VMEM((MIX_ROWS, d), _BF16),
            pltpu.VMEM((MIX_ROWS, d), _BF16),
            pltpu.VMEM(_hist_shape(MIX_ROWS, CONV_DIM), _F32),
            pltpu.VMEM(_hist_shape(MIX_ROWS, POOL_DIM), _F32),
        ],
        compiler_params=pltpu.CompilerParams(
            dimension_semantics=("arbitrary",), vmem_limit_bytes=VMEM_LIMIT_BYTES),
        name="mixer",
    )(x2d, x2d, *[arr for arr, _ in params])


def kernel(x, ffn1_norm, ffn1_w_gate, ffn1_w_up, ffn1_w_down, mix_norm, w_in, sgu_ln_g, sgu_ln_b, w_spatial, b_spatial, conv_w, conv_b, conv_ln_g, conv_ln_b, pool_w, pool_scale, w_out, ffn2_norm, ffn2_w_gate, ffn2_w_up, ffn2_w_down, final_norm):
    batch, seq, d = x.shape
    depth = ffn1_norm.shape[0]
    h = x.reshape(batch * seq, d)
    b_sp_t = jnp.repeat(jnp.swapaxes(b_spatial, 1, 2), HEAD_DIM, axis=2)
    groups = pool_w.shape[1]
    pool_w_bd = jnp.einsum("lgrc,gh->lgrhc", pool_w, jnp.eye(groups, dtype=pool_w.dtype))
    pool_w_bd = pool_w_bd.reshape(depth, POOL_DIM, POOL_DIM).astype(_BF16)
    ffn1_w = _cast_weights([ffn1_w_gate, ffn1_w_up, ffn1_w_down], 0)
    for l in range(depth):
        later = [(w_in, l), (w_out, l), (ffn2_w_gate, l), (ffn2_w_up, l), (ffn2_w_down, l)]
        h, (w_in_l, w_out_l, *ffn2_w) = _ffn(h, ffn1_norm, l, *ffn1_w, cast_next=later)
        h = _mixer(h, seq, l, mix_norm, w_in_l, sgu_ln_g, sgu_ln_b, w_spatial, b_sp_t,
                   conv_w, conv_b, conv_ln_g, conv_ln_b, pool_w_bd, pool_scale, w_out_l)
        last = l == depth - 1
        later = [] if last else [(ffn1_w_gate, l + 1), (ffn1_w_up, l + 1), (ffn1_w_down, l + 1)]
        h, ffn1_w = _ffn(h, ffn2_norm, l, *ffn2_w, final_g=final_norm if last else None, cast_next=later)
    return h.reshape(batch, seq, d)
```
